```python
import math
import jax, jax.numpy as jnp
from jax import lax
import numpy as np

D_MODEL = 1024
BATCH = 8
SEQ = 2048
DEPTH = 2

CHUNK = 64
Q_BLOCK = 128
N_A = DEPTH // 2
N_B = DEPTH - N_A
NORM_EPS = 1e-6

MLA_HEADS = D_MODEL // 128
MLA_NOPE = 128
MLA_ROPE = 64
MLA_V = 128
MLA_Q_LORA = (3 * D_MODEL) // 4
MLA_KV_LORA = D_MODEL // 2
MLA_THETA = 10000.0
MLA_GATE = MLA_HEADS * MLA_V
MLA_IN = MLA_Q_LORA + MLA_KV_LORA + MLA_ROPE + MLA_GATE

DIFF_HEADS = D_MODEL // 128
DIFF_QK = 64
DIFF_V = 2 * DIFF_QK
DIFF_ROT = DIFF_QK // 4
DIFF_THETA = 500000.0
DIFF_Q_WIDTH = DIFF_HEADS * 2 * DIFF_QK
DIFF_GATE = DIFF_HEADS * DIFF_V
DIFF_IN = DIFF_Q_WIDTH + DIFF_GATE
DIFF_KV = DIFF_HEADS * 2 * DIFF_QK + DIFF_HEADS * DIFF_V

kernel_name = "yoco_mla_diffattn_streaming_hybrid"


def rmsnorm(x, g):
    xf = x.astype(jnp.float32)
    y = xf * lax.rsqrt(jnp.mean(xf * xf, axis=-1, keepdims=True) + NORM_EPS)
    return (y * g.astype(jnp.float32)).astype(x.dtype)


def rope_tables(positions, rot_dim, theta):
    inv = theta ** (-jnp.arange(0, rot_dim, 2, dtype=jnp.float32) / rot_dim)
    ang = positions.astype(jnp.float32)[..., None] * inv
    return jnp.cos(ang), jnp.sin(ang)


def apply_rope(x, cos, sin):
    r = cos.shape[-1]
    cos = cos.astype(x.dtype)
    sin = sin.astype(x.dtype)
    x1, x2, rest = x[..., :r], x[..., r:2 * r], x[..., 2 * r:]
    return jnp.concatenate([x1 * cos - x2 * sin, x2 * cos + x1 * sin, rest], axis=-1)


def chunk_mask(blk, seq_len):
    qpos = blk * Q_BLOCK + jnp.arange(Q_BLOCK)
    kpos = jnp.arange(seq_len)
    return (kpos[None, :] // CHUNK) <= (qpos[:, None] // CHUNK)


def masked_softmax(s, mask):
    return jax.nn.softmax(jnp.where(mask[None, None], s, -jnp.inf), axis=-1)


def sweep_query_blocks(fn, seq_len, *qs):
    nb = seq_len // Q_BLOCK
    blocks = tuple(jnp.moveaxis(a.reshape(a.shape[0], nb, Q_BLOCK, *a.shape[2:]), 1, 0) for a in qs)
    out = lax.map(lambda args: fn(args[0], *args[1]), (jnp.arange(nb), blocks))
    out = jnp.moveaxis(out, 0, 1)
    return out.reshape(out.shape[0], seq_len, *out.shape[3:])


def mla_layer(h, cos, sin, pre_g, w_in, q_norm_g, w_uq, kv_norm_g, w_uk, w_uv, w_o, post_g):
    B, S, _ = h.shape
    u = rmsnorm(h, pre_g)
    proj = u @ w_in
    c_q, c_kv, k_r, z = jnp.split(
        proj, [MLA_Q_LORA, MLA_Q_LORA + MLA_KV_LORA, MLA_Q_LORA + MLA_KV_LORA + MLA_ROPE], axis=-1)
    q = (rmsnorm(c_q, q_norm_g) @ w_uq).reshape(B, S, MLA_HEADS, MLA_NOPE + MLA_ROPE)
    q_n = q[..., :MLA_NOPE]
    q_r = apply_rope(q[..., MLA_NOPE:], cos[:, :, None], sin[:, :, None])
    c_kv = rmsnorm(c_kv, kv_norm_g)
    k_n = (c_kv @ w_uk).reshape(B, S, MLA_HEADS, MLA_NOPE)
    v = (c_kv @ w_uv).reshape(B, S, MLA_HEADS, MLA_V)
    k_r = apply_rope(k_r, cos, sin)
    scale = (MLA_NOPE + MLA_ROPE) ** -0.5

    def attend(blk, qn, qr):
        s = (jnp.einsum('bqhd,bkhd->bhqk', qn, k_n)
             + jnp.einsum('bqhr,bkr->bhqk', qr, k_r)).astype(jnp.float32) * scale
        p = masked_softmax(s, chunk_mask(blk, S)).astype(v.dtype)
        return jnp.einsum('bhqk,bkhe->bqhe', p, v)

    o = sweep_query_blocks(attend, S, q_n, q_r).reshape(B, S, MLA_HEADS * MLA_V)
    o = (o * jax.nn.silu(z)) @ w_o
    return h + rmsnorm(o, post_g)


def shared_kv(h, cos, sin, kv_norm_g, w_kv):
    B, S, _ = h.shape
    u = rmsnorm(h, kv_norm_g)
    k, v = jnp.split(u @ w_kv, [DIFF_HEADS * 2 * DIFF_QK], axis=-1)
    k = apply_rope(k.reshape(B, S, DIFF_HEADS, 2, DIFF_QK), cos[:, :, None, None], sin[:, :, None, None])
    return k[..., 0, :], k[..., 1, :], v.reshape(B, S, DIFF_HEADS, DIFF_V)


def diff_layer(h, layer_idx, cos, sin, k1, k2, v, pre_g, w_in, lam, subln_g, w_o, post_g):
    B, S, _ = h.shape
    u = rmsnorm(h, pre_g)
    q, z = jnp.split(u @ w_in, [DIFF_Q_WIDTH], axis=-1)
    q = apply_rope(q.reshape(B, S, DIFF_HEADS, 2, DIFF_QK), cos[:, :, None, None], sin[:, :, None, None])
    q1, q2 = q[..., 0, :], q[..., 1, :]
    lam_init = 0.8 - 0.6 * math.exp(-0.3 * layer_idx)
    lf = lam.astype(jnp.float32)
    lam_full = jnp.exp(jnp.sum(lf[0] * lf[1])) - jnp.exp(jnp.sum(lf[2] * lf[3])) + lam_init
    scale = DIFF_QK ** -0.5

    def attend(blk, qa, qb):
        mask = chunk_mask(blk, S)
        s1 = jnp.einsum('bqhd,bkhd->bhqk', qa, k1).astype(jnp.float32) * scale
        s2 = jnp.einsum('bqhd,bkhd->bhqk', qb, k2).astype(jnp.float32) * scale
        p = masked_softmax(s1, mask) - lam_full * masked_softmax(s2, mask)
        return jnp.einsum('bhqk,bkhe->bqhe', p.astype(v.dtype), v)

    o = sweep_query_blocks(attend, S, q1, q2)
    o = rmsnorm(o, subln_g) * (1.0 - lam_init)
    o = (o.reshape(B, S, DIFF_HEADS * DIFF_V) * jax.nn.silu(z)) @ w_o
    return h + rmsnorm(o, post_g)


def setup_inputs(seed: int = 0) -> dict:
    key = jax.random.key(seed)
    ks = jax.random.split(key, 24)
    f32 = jnp.float32

    def w(k, shape, fan_in):
        return jax.random.normal(k, shape, f32) * (fan_in ** -0.5)

    def gain(k, shape):
        return 1.0 + 0.02 * jax.random.normal(k, shape, f32)

    x = jax.random.normal(ks[0], (BATCH, SEQ, D_MODEL), f32)
    start = jax.random.randint(ks[1], (BATCH, 1), 0, 4096, dtype=jnp.int32)
    positions = start + jnp.arange(SEQ, dtype=jnp.int32)[None, :]
    return {
        "x": x,
        "positions": positions,
        "a_pre_g": gain(ks[2], (N_A, D_MODEL)),
        "a_w_in": w(ks[3], (N_A, D_MODEL, MLA_IN), D_MODEL),
        "a_q_norm_g": gain(ks[4], (N_A, MLA_Q_LORA)),
        "a_w_uq": w(ks[5], (N_A, MLA_Q_LORA, MLA_HEADS * (MLA_NOPE + MLA_ROPE)), MLA_Q_LORA),
        "a_kv_norm_g": gain(ks[6], (N_A, MLA_KV_LORA)),
        "a_w_uk": w(ks[7], (N_A, MLA_KV_LORA, MLA_HEADS * MLA_NOPE), MLA_KV_LORA),
        "a_w_uv": w(ks[8], (N_A, MLA_KV_LORA, MLA_HEADS * MLA_V), MLA_KV_LORA),
        "a_w_o": w(ks[9], (N_A, MLA_HEADS * MLA_V, D_MODEL), MLA_HEADS * MLA_V),
        "a_post_g": gain(ks[10], (N_A, D_MODEL)),
        "b_kv_norm_g": gain(ks[11], (D_MODEL,)),
        "b_w_kv": w(ks[12], (D_MODEL, DIFF_KV), D_MODEL),
        "b_pre_g": gain(ks[13], (N_B, D_MODEL)),
        "b_w_in": w(ks[14], (N_B, D_MODEL, DIFF_IN), D_MODEL),
        "b_lambda": 0.1 * jax.random.normal(ks[15], (N_B, 4, DIFF_QK), f32),
        "b_subln_g": gain(ks[16], (N_B, DIFF_V)),
        "b_w_o": w(ks[17], (N_B, DIFF_HEADS * DIFF_V, D_MODEL), DIFF_HEADS * DIFF_V),
        "b_post_g": gain(ks[18], (N_B, D_MODEL)),
    }


def reference(x, positions, a_pre_g, a_w_in, a_q_norm_g, a_w_uq, a_kv_norm_g, a_w_uk, a_w_uv,
              a_w_o, a_post_g, b_kv_norm_g, b_w_kv, b_pre_g, b_w_in, b_lambda, b_subln_g,
              b_w_o, b_post_g):
    cos_a, sin_a = rope_tables(positions, MLA_ROPE, MLA_THETA)
    cos_b, sin_b = rope_tables(positions, DIFF_ROT, DIFF_THETA)
    h = x
    k1 = k2 = v = None
    for layer in range(DEPTH):
        if layer < N_A:
            h = mla_layer(h, cos_a, sin_a, a_pre_g[layer], a_w_in[layer], a_q_norm_g[layer],
                          a_w_uq[layer], a_kv_norm_g[layer], a_w_uk[layer], a_w_uv[layer],
                          a_w_o[layer], a_post_g[layer])
        else:
            if layer == N_A:
                k1, k2, v = shared_kv(h, cos_b, sin_b, b_kv_norm_g, b_w_kv)
            j = layer - N_A
            h = diff_layer(h, layer, cos_b, sin_b, k1, k2, v, b_pre_g[j], b_w_in[j], b_lambda[j],
                           b_subln_g[j], b_w_o[j], b_post_g[j])
    return h
```

```python
import functools
import math

import jax
import jax.numpy as jnp
from jax import lax
from jax.experimental import pallas as pl
from jax.experimental.pallas import tpu as pltpu

F32 = jnp.float32
BF16 = jnp.bfloat16

D_MODEL = 1024
CHUNK = 64
NORM_EPS = 1e-6

HEADS = 8
MLA_NOPE = 128
MLA_ROPE = 64
MLA_V = 128
MLA_Q_LORA = 768
MLA_KV_LORA = 512
MLA_THETA = 10000.0
MLA_GATE = HEADS * MLA_V

DIFF_QK = 64
DIFF_V = 128
DIFF_ROT = 16
DIFF_THETA = 500000.0
DIFF_LAYER_IDX = 1
LAM_INIT = 0.8 - 0.6 * math.exp(-0.3 * DIFF_LAYER_IDX)

LANES = 128
ROPE_GROUP = 64
LOG2E = math.log2(math.e)

TOKEN_TILE = 512
ATT_BLOCK = 256
VMEM_LIMIT_BYTES = 56 * 1024 * 1024


def _rms_normalize(x):
    return x * lax.rsqrt(jnp.mean(x * x, axis=-1, keepdims=True) + NORM_EPS)


def _rope_tables(pos_f32, inv_row, half):
    ang = pos_f32 * inv_row
    cos = jnp.cos(ang)
    sin = jnp.sin(ang)
    j = lax.broadcasted_iota(jnp.int32, ang.shape, 1) & (ROPE_GROUP - 1)
    c_tab = jnp.where(j < 2 * half, cos, 1.0)
    s1_tab = jnp.where(j < half, -sin, 0.0)
    s2_tab = jnp.where((j >= half) & (j < 2 * half), sin, 0.0)
    return c_tab, s1_tab, s2_tab


def _rope(x, tabs, half):
    c_tab, s1_tab, s2_tab = tabs
    return (x * c_tab
            + pltpu.roll(x, LANES - half, 1) * s1_tab
            + pltpu.roll(x, half, 1) * s2_tab)


def _silu(z):
    return z * (1.0 / (1.0 + jnp.exp(-z)))


def _mla_in_kernel(x_ref, pos_ref, inv_ref, preg_ref, win_ref, qg_ref, wuq_ref, kvg_ref, wuk_ref,
                   wuvt_ref, q_out, k_out, vt_out, gz_out):
    x = x_ref[...]
    u = (_rms_normalize(x) * preg_ref[...]).astype(BF16)
    proj = jnp.dot(u, win_ref[...], preferred_element_type=F32)
    o_kv = MLA_Q_LORA
    o_kr = o_kv + MLA_KV_LORA
    o_z = o_kr + LANES
    cq = proj[:, :o_kv]
    ckv = proj[:, o_kv:o_kr]
    kr = proj[:, o_kr:o_z]
    z = proj[:, o_z:]

    tabs = _rope_tables(pos_ref[...].astype(F32), inv_ref[...], MLA_ROPE // 2)
    q_scale = (MLA_NOPE + MLA_ROPE) ** -0.5 * LOG2E

    cqn = (_rms_normalize(cq) * qg_ref[...]).astype(BF16)
    q = jnp.dot(cqn, wuq_ref[...], preferred_element_type=F32) * q_scale
    ckvn = (_rms_normalize(ckv) * kvg_ref[...]).astype(BF16)
    kn = jnp.dot(ckvn, wuk_ref[...], preferred_element_type=F32)
    krf = _rope(kr, tabs, MLA_ROPE // 2).astype(BF16)

    n_nope = HEADS * MLA_NOPE
    for pair in range(HEADS // 2):
        qr = _rope(q[:, n_nope + pair * LANES:n_nope + (pair + 1) * LANES], tabs, MLA_ROPE // 2)
        qr_swapped = pltpu.roll(qr, MLA_ROPE, 1)
        for h, rope_part in ((2 * pair, qr), (2 * pair + 1, qr_swapped)):
            base = h * 2 * LANES
            q_out[:, base:base + LANES] = q[:, h * MLA_NOPE:(h + 1) * MLA_NOPE].astype(BF16)
            q_out[:, base + LANES:base + 2 * LANES] = rope_part.astype(BF16)
    for h in range(HEADS):
        base = h * 2 * LANES
        k_out[:, base:base + LANES] = kn[:, h * MLA_NOPE:(h + 1) * MLA_NOPE].astype(BF16)
        k_out[:, base + LANES:base + 2 * LANES] = krf

    vt = lax.dot_general(wuvt_ref[...], ckvn, (((1,), (1,)), ((), ())),
                         preferred_element_type=F32)
    vt_out[...] = vt.astype(BF16)
    gz_out[...] = _silu(z).astype(BF16)


def _diag_mask():
    kk = lax.broadcasted_iota(jnp.int32, (ATT_BLOCK, ATT_BLOCK), 0) // CHUNK
    qq = lax.broadcasted_iota(jnp.int32, (ATT_BLOCK, ATT_BLOCK), 1) // CHUNK
    return kk <= qq


def _scores_t(k_ref, q, s_ref, qi, mask):
    nt = (((1,), (1,)), ((), ()))
    lo = qi * ATT_BLOCK
    hi = lo + ATT_BLOCK
    if qi > 0:
        s_ref[0:lo, :] = lax.dot_general(k_ref[0:lo, :], q, nt, preferred_element_type=F32)
    sd = lax.dot_general(k_ref[lo:hi, :], q, nt, preferred_element_type=F32)
    s_ref[lo:hi, :] = jnp.where(mask, sd, -jnp.inf)
    return hi


def _softmax_t(s_ref, p_ref, length):
    s = s_ref[0:length, :]
    m = jnp.max(s, axis=0, keepdims=True)
    p = jnp.exp2(s - m)
    p_ref[0:length, :] = p.astype(BF16)
    return 1.0 / jnp.sum(p, axis=0, keepdims=True)


def _mla_attn_kernel(q_ref, k_ref, vt_ref, gz_ref, o_ref, s_ref, p_ref):
    mask = _diag_mask()
    n_blocks = q_ref.shape[0] // ATT_BLOCK
    for qi in range(n_blocks):
        rows = slice(qi * ATT_BLOCK, (qi + 1) * ATT_BLOCK)
        length = _scores_t(k_ref, q_ref[rows, :], s_ref, qi, mask)
        inv_l = _softmax_t(s_ref, p_ref, length)
        ot = jnp.dot(vt_ref[:, 0:length], p_ref[0:length, :], preferred_element_type=F32)
        o = (ot * inv_l).T
        o_ref[rows, :] = (o * gz_ref[rows, :].astype(F32)).astype(BF16)


def _mid_kernel(og_ref, x_ref, pos_ref, inv_ref, wo_ref, postg_ref, kvg_ref, wk_ref, wvt_ref,
                preg_ref, win_ref, h_out, q_out, k1_out, k2_out, vt_out, gz_out):
    o = jnp.dot(og_ref[...], wo_ref[...], preferred_element_type=F32)
    h = x_ref[...] + _rms_normalize(o) * postg_ref[...]
    h_out[...] = h

    hn = _rms_normalize(h)
    u_kv = (hn * kvg_ref[...]).astype(BF16)
    u_q = (hn * preg_ref[...]).astype(BF16)
    tabs = _rope_tables(pos_ref[...].astype(F32), inv_ref[...], DIFF_ROT // 2)
    lane = lax.broadcasted_iota(jnp.int32, (h.shape[0], LANES), 1)
    first = lane < DIFF_QK

    k = jnp.dot(u_kv, wk_ref[...], preferred_element_type=F32)
    qz = jnp.dot(u_q, win_ref[...], preferred_element_type=F32)
    q_scale = DIFF_QK ** -0.5 * LOG2E
    for h_i in range(HEADS):
        cols = slice(h_i * LANES, (h_i + 1) * LANES)
        kh = _rope(k[:, cols], tabs, DIFF_ROT // 2)
        k1_out[:, cols] = jnp.where(first, kh, 0.0).astype(BF16)
        k2_out[:, cols] = jnp.where(first, 0.0, kh).astype(BF16)
        q_out[:, cols] = (_rope(qz[:, cols], tabs, DIFF_ROT // 2) * q_scale).astype(BF16)
    vt = lax.dot_general(wvt_ref[...], u_kv, (((1,), (1,)), ((), ())),
                         preferred_element_type=F32)
    vt_out[...] = vt.astype(BF16)
    gz_out[...] = _silu(qz[:, HEADS * 2 * DIFF_QK:]).astype(BF16)


def _diff_attn_kernel(lam_ref, g_ref, q_ref, k1_ref, k2_ref, vt_ref, gz_ref, o_ref,
                      s_ref, p1_ref, p2_ref):
    lam = lam_ref[...]
    lam_full = (jnp.exp(jnp.sum(lam[0:1, :] * lam[1:2, :], axis=1, keepdims=True))
                - jnp.exp(jnp.sum(lam[2:3, :] * lam[3:4, :], axis=1, keepdims=True))
                + LAM_INIT)
    g_col = g_ref[...] * (1.0 - LAM_INIT)
    mask = _diag_mask()
    n_blocks = q_ref.shape[0] // ATT_BLOCK
    for qi in range(n_blocks):
        rows = slice(qi * ATT_BLOCK, (qi + 1) * ATT_BLOCK)
        q = q_ref[rows, :]
        length = _scores_t(k1_ref, q, s_ref, qi, mask)
        inv_l1 = _softmax_t(s_ref, p1_ref, length)
        length = _scores_t(k2_ref, q, s_ref, qi, mask)
        inv_l2 = _softmax_t(s_ref, p2_ref, length)
        vt = vt_ref[:, 0:length]
        ot1 = jnp.dot(vt, p1_ref[0:length, :], preferred_element_type=F32)
        ot2 = jnp.dot(vt, p2_ref[0:length, :], preferred_element_type=F32)
        ot = ot1 * inv_l1 - ot2 * (lam_full * inv_l2)
        ot = ot * lax.rsqrt(jnp.mean(ot * ot, axis=0, keepdims=True) + NORM_EPS) * g_col
        o_ref[rows, :] = (ot.T * gz_ref[rows, :].astype(F32)).astype(BF16)


def _out_kernel(og_ref, h_ref, wo_ref, postg_ref, y_out):
    o = jnp.dot(og_ref[...], wo_ref[...], preferred_element_type=F32)
    y_out[...] = h_ref[...] + _rms_normalize(o) * postg_ref[...]


def _row_spec(width):
    return pl.BlockSpec((TOKEN_TILE, width), lambda i: (i, 0))


def _resident(shape):
    return pl.BlockSpec(shape, lambda *_: (0,) * len(shape), pipeline_mode=pl.Buffered(1))


def _params(n_axes):
    return pltpu.CompilerParams(dimension_semantics=("parallel",) * n_axes,
                                vmem_limit_bytes=VMEM_LIMIT_BYTES)


def _inv_freq_row(rot_dim, theta):
    inv = theta ** (-jnp.arange(0, rot_dim, 2, dtype=F32) / rot_dim)
    group = jnp.concatenate([inv, inv, jnp.zeros((ROPE_GROUP - rot_dim,), F32)])
    return jnp.tile(group, LANES // group.shape[0])[None, :]


def kernel(x, positions, a_pre_g, a_w_in, a_q_norm_g, a_w_uq, a_kv_norm_g, a_w_uk, a_w_uv, a_w_o,
           a_post_g, b_kv_norm_g, b_w_kv, b_pre_g, b_w_in, b_lambda, b_subln_g, b_w_o, b_post_g):
    batch, seq, d_model = x.shape
    assert d_model == D_MODEL and seq % ATT_BLOCK == 0 and (batch * seq) % TOKEN_TILE == 0
    tokens = batch * seq
    n_tiles = tokens // TOKEN_TILE
    x2 = x.reshape(tokens, d_model)
    pos = positions.reshape(tokens, 1)

    wi = a_w_in[0]
    o_kr = MLA_Q_LORA + MLA_KV_LORA
    w_in_a = jnp.concatenate(
        [wi[:, :o_kr + MLA_ROPE], jnp.zeros((d_model, LANES - MLA_ROPE), F32), wi[:, o_kr + MLA_ROPE:]],
        axis=1).astype(BF16)
    wuq = a_w_uq[0].reshape(MLA_Q_LORA, HEADS, MLA_NOPE + MLA_ROPE)
    w_uq_a = jnp.concatenate([wuq[:, :, :MLA_NOPE].reshape(MLA_Q_LORA, HEADS * MLA_NOPE),
                              wuq[:, :, MLA_NOPE:].reshape(MLA_Q_LORA, HEADS * MLA_ROPE)],
                             axis=1).astype(BF16)
    w_uk_a = a_w_uk[0].astype(BF16)
    w_uvt_a = a_w_uv[0].T.astype(BF16)
    w_o_a = a_w_o[0].astype(BF16)
    n_k = HEADS * 2 * DIFF_QK
    w_k_b = b_w_kv[:, :n_k].astype(BF16)
    w_vt_b = b_w_kv[:, n_k:].T.astype(BF16)
    w_in_b = b_w_in[0].astype(BF16)
    w_o_b = b_w_o[0].astype(BF16)
    inv_a = _inv_freq_row(MLA_ROPE, MLA_THETA)
    inv_b = _inv_freq_row(DIFF_ROT, DIFF_THETA)
    row = lambda v: v.reshape(1, -1)

    wide = HEADS * 2 * LANES
    q_a, k_a, vt_a, gz_a = pl.pallas_call(
        _mla_in_kernel,
        grid=(n_tiles,),
        in_specs=[_row_spec(d_model), _row_spec(1), _resident((1, LANES)), _resident((1, d_model)),
                  _resident(w_in_a.shape), _resident((1, MLA_Q_LORA)), _resident(w_uq_a.shape),
                  _resident((1, MLA_KV_LORA)), _resident(w_uk_a.shape), _resident(w_uvt_a.shape)],
        out_specs=[_row_spec(wide), _row_spec(wide),
                   pl.BlockSpec((MLA_GATE, TOKEN_TILE), lambda i: (0, i)), _row_spec(MLA_GATE)],
        out_shape=[jax.ShapeDtypeStruct((tokens, wide), BF16), jax.ShapeDtypeStruct((tokens, wide), BF16),
                   jax.ShapeDtypeStruct((MLA_GATE, tokens), BF16),
                   jax.ShapeDtypeStruct((tokens, MLA_GATE), BF16)],
        compiler_params=_params(1),
        name="mla_in",
    )(x2, pos, inv_a, row(a_pre_g[0]), w_in_a, row(a_q_norm_g[0]), w_uq_a, row(a_kv_norm_g[0]),
      w_uk_a, w_uvt_a)

    head_rows = lambda width: pl.BlockSpec((seq, width), lambda b, h: (b, h))
    vt_spec = pl.BlockSpec((MLA_V, seq), lambda b, h: (h, b))
    og_a = pl.pallas_call(
        _mla_attn_kernel,
        grid=(batch, HEADS),
        in_specs=[head_rows(2 * LANES), head_rows(2 * LANES), vt_spec, head_rows(MLA_V)],
        out_specs=head_rows(MLA_V),
        out_shape=jax.ShapeDtypeStruct((tokens, MLA_GATE), BF16),
        scratch_shapes=[pltpu.VMEM((seq, ATT_BLOCK), F32), pltpu.VMEM((seq, ATT_BLOCK), BF16)],
        compiler_params=_params(2),
        name="mla_attn",
    )(q_a, k_a, vt_a, gz_a)

    h1, q_b, k1_b, k2_b, vt_b, gz_b = pl.pallas_call(
        _mid_kernel,
        grid=(n_tiles,),
        in_specs=[_row_spec(MLA_GATE), _row_spec(d_model), _row_spec(1), _resident((1, LANES)),
                  _resident(w_o_a.shape), _resident((1, d_model)), _resident((1, d_model)),
                  _resident(w_k_b.shape), _resident(w_vt_b.shape), _resident((1, d_model)),
                  _resident(w_in_b.shape)],
        out_specs=[_row_spec(d_model), _row_spec(n_k), _row_spec(n_k), _row_spec(n_k),
                   pl.BlockSpec((HEADS * DIFF_V, TOKEN_TILE), lambda i: (0, i)), _row_spec(HEADS * DIFF_V)],
        out_shape=[jax.ShapeDtypeStruct((tokens, d_model), F32), jax.ShapeDtypeStruct((tokens, n_k), BF16),
                   jax.ShapeDtypeStruct((tokens, n_k), BF16), jax.ShapeDtypeStruct((tokens, n_k), BF16),
                   jax.ShapeDtypeStruct((HEADS * DIFF_V, tokens), BF16),
                   jax.ShapeDtypeStruct((tokens, HEADS * DIFF_V), BF16)],
        compiler_params=_params(1),
        name="mid",
    )(og_a, x2, pos, inv_b, w_o_a, row(a_post_g[0]), row(b_kv_norm_g), w_k_b, w_vt_b,
      row(b_pre_g[0]), w_in_b)

    og_b = pl.pallas_call(
        _diff_attn_kernel,
        grid=(batch, HEADS),
        in_specs=[pl.BlockSpec((4, DIFF_QK), lambda b, h: (0, 0)),
                  pl.BlockSpec((DIFF_V, 1), lambda b, h: (0, 0)),
                  head_rows(LANES), head_rows(LANES), head_rows(LANES), vt_spec, head_rows(DIFF_V)],
        out_specs=head_rows(DIFF_V),
        out_shape=jax.ShapeDtypeStruct((tokens, HEADS * DIFF_V), BF16),
        scratch_shapes=[pltpu.VMEM((seq, ATT_BLOCK), F32), pltpu.VMEM((seq, ATT_BLOCK), BF16),
                        pltpu.VMEM((seq, ATT_BLOCK), BF16)],
        compiler_params=_params(2),
        name="diff_attn",
    )(b_lambda[0], b_subln_g[0].reshape(DIFF_V, 1), q_b, k1_b, k2_b, vt_b, gz_b)

    y = pl.pallas_call(
        _out_kernel,
        grid=(n_tiles,),
        in_specs=[_row_spec(HEADS * DIFF_V), _row_spec(d_model), _resident(w_o_b.shape),
                  _resident((1, d_model))],
        out_specs=_row_spec(d_model),
        out_shape=jax.ShapeDtypeStruct((tokens, d_model), F32),
        compiler_params=_params(1),
        name="out_proj",
    )(og_b, h1, w_o_b, row(b_post_g[0]))
    return y.reshape(batch, seq, d_model)
```
